```python
import math
import jax
import jax.numpy as jnp
from jax import lax
import numpy as np

D_MODEL = 2048
BATCH = 2
SEQ = 4096
DEPTH = 2
DEC_BATCH = 16
DEC_SEQ = 64
PAST_LEN = 2048

CHUNK = 64
Q_BLOCK = 128
N_MIXERS = 2
N_ATTN_LAYERS = (DEPTH + 1) // 2
N_DELTA_LAYERS = DEPTH // 2
EPS = 1e-6
A_HEADS = 16
A_QK_DIM = 64
A_V_DIM = 2 * A_QK_DIM
A_QK_WIDTH = A_HEADS * 2 * A_QK_DIM
A_V_WIDTH = A_HEADS * A_V_DIM
A_IN_WIDTH = 2 * A_QK_WIDTH + A_V_WIDTH
G_K_HEADS = 16
G_V_HEADS = 32
G_K_DIM = 128
G_V_DIM = 128
G_CONV = 4
G_QK_WIDTH = G_K_HEADS * G_K_DIM
G_V_WIDTH = G_V_HEADS * G_V_DIM
G_CONV_WIDTH = 2 * G_QK_WIDTH + G_V_WIDTH
G_IN_WIDTH = G_CONV_WIDTH + G_V_WIDTH + 2 * G_V_HEADS
N_GROUPS = 8
EXPERTS_PER_GROUP = 8
N_EXPERTS = N_GROUPS * EXPERTS_PER_GROUP
TOP_K = 2
D_EXPERT = 512

kernel_name = "hybrid_diffattn_gdn_hmoe_stream_step"


def rms_norm(x, gain):
    xf = x.astype(jnp.float32)
    y = xf * lax.rsqrt(jnp.mean(xf * xf, axis=-1, keepdims=True) + EPS)
    return (y * gain.astype(jnp.float32)).astype(x.dtype)


def l2_normalize(x):
    xf = x.astype(jnp.float32)
    return xf * lax.rsqrt(jnp.sum(xf * xf, axis=-1, keepdims=True) + EPS)


def diff_lambda(lam_params, lam_init):
    lp = lam_params.astype(jnp.float32)
    return jnp.exp(jnp.sum(lp[0] * lp[1])) - jnp.exp(jnp.sum(lp[2] * lp[3])) + lam_init


def diff_attn_project(h, w_in, q_gain, k_gain):
    b, t, _ = h.shape
    q, k, v = jnp.split(h @ w_in, [A_QK_WIDTH, 2 * A_QK_WIDTH], axis=-1)
    q = rms_norm(q.reshape(b, t, A_HEADS, 2, A_QK_DIM), q_gain)
    k = rms_norm(k.reshape(b, t, A_HEADS, 2, A_QK_DIM), k_gain)
    return q, k, v.reshape(b, t, A_HEADS, A_V_DIM)


def diff_attention_core(q, k, v, lam, q_offset):
    b, tq = q.shape[:2]
    tk = k.shape[1]
    blk = min(Q_BLOCK, tq)
    nblk = tq // blk
    k_chunk = jnp.arange(tk) // CHUNK
    scale = A_QK_DIM ** -0.5
    qb = jnp.moveaxis(q.reshape(b, nblk, blk, A_HEADS, 2, A_QK_DIM), 1, 0)

    def block(args):
        q_blk, b_idx = args
        q_chunk = (q_offset + b_idx * blk + jnp.arange(blk)) // CHUNK
        mask = k_chunk[None, :] <= q_chunk[:, None]
        s = jnp.einsum('bqhcd,bkhcd->bhcqk', q_blk, k).astype(jnp.float32) * scale
        p = jax.nn.softmax(jnp.where(mask, s, -jnp.inf), axis=-1)
        w = p[:, :, 0] - lam * p[:, :, 1]
        return jnp.einsum('bhqk,bkhe->bqhe', w.astype(v.dtype), v)

    out = lax.map(block, (qb, jnp.arange(nblk)))
    return jnp.moveaxis(out, 0, 1).reshape(b, tq, A_HEADS, A_V_DIM)


def diff_attn_output(o, sub_gain, lam_init, w_out):
    b, t = o.shape[:2]
    o = rms_norm(o, sub_gain) * (1.0 - lam_init)
    return o.reshape(b, t, A_V_WIDTH) @ w_out


def causal_conv(u, hist, w):
    t = u.shape[1]
    up = jnp.concatenate([hist, u], axis=1)
    y = up[:, 0:t] * w[0]
    for j in range(1, G_CONV):
        y = y + up[:, j:j + t] * w[j]
    return jax.nn.silu(y), up[:, -(G_CONV - 1):]


def gated_delta_rule(q, k, v, g, beta, state):
    b, t, h, _ = q.shape
    L = min(CHUNK, t)
    n = t // L

    def chunks(a):
        a = a.reshape(b, n, L, h, *a.shape[3:])
        return jnp.moveaxis(jnp.moveaxis(a, 1, 0), 3, 2)

    qc, kc, vc, bc = chunks(q), chunks(k), chunks(v), chunks(beta)
    gc = jnp.cumsum(chunks(g), axis=-1)
    idx = jnp.arange(L)
    incl = idx[:, None] >= idx[None, :]
    strict = idx[:, None] > idx[None, :]
    decay = jnp.exp(jnp.where(incl, gc[..., :, None] - gc[..., None, :], -jnp.inf))
    kk = jnp.einsum('nbhid,nbhjd->nbhij', kc, kc)
    a_mat = jnp.where(strict, bc[..., :, None] * kk * decay, 0.0) + jnp.eye(L, dtype=jnp.float32)
    u = lax.linalg.triangular_solve(a_mat, vc * bc[..., None], left_side=True, lower=True,
                                    unit_diagonal=True)
    w = lax.linalg.triangular_solve(a_mat, kc * (bc * jnp.exp(gc))[..., None], left_side=True,
                                    lower=True, unit_diagonal=True)
    qk = jnp.einsum('nbhid,nbhjd->nbhij', qc, kc) * decay
    q_dec = qc * jnp.exp(gc)[..., None]
    k_dec = kc * jnp.exp(gc[..., -1:] - gc)[..., None]
    g_last = jnp.exp(gc[..., -1])

    def step(s, inp):
        u_n, w_n, qk_n, qd_n, kd_n, gl_n = inp
        v_new = u_n - jnp.einsum('bhld,bhde->bhle', w_n, s)
        o_n = jnp.einsum('bhld,bhde->bhle', qd_n, s) + jnp.einsum('bhij,bhje->bhie', qk_n, v_new)
        s = s * gl_n[..., None, None] + jnp.einsum('bhld,bhle->bhde', kd_n, v_new)
        return s, o_n

    s_final, o = lax.scan(step, state, (u, w, qk, q_dec, k_dec, g_last))
    o = jnp.swapaxes(jnp.moveaxis(o, 0, 1), 2, 3).reshape(b, t, h, -1)
    return o, s_final


def gdn_mixer(h, conv_hist, s0, w_in, conv_w, a_log, dt_bias, out_gain, w_out):
    b, t, _ = h.shape
    u, z, bt, al = jnp.split(h @ w_in, [G_CONV_WIDTH, G_CONV_WIDTH + G_V_WIDTH,
                                         G_CONV_WIDTH + G_V_WIDTH + G_V_HEADS], axis=-1)
    c, new_hist = causal_conv(u, conv_hist.astype(u.dtype), conv_w)
    q, k, v = jnp.split(c, [G_QK_WIDTH, 2 * G_QK_WIDTH], axis=-1)
    rep = G_V_HEADS // G_K_HEADS
    q = jnp.repeat(l2_normalize(q.reshape(b, t, G_K_HEADS, G_K_DIM)) * (G_K_DIM ** -0.5), rep, axis=2)
    k = jnp.repeat(l2_normalize(k.reshape(b, t, G_K_HEADS, G_K_DIM)), rep, axis=2)
    v = v.reshape(b, t, G_V_HEADS, G_V_DIM).astype(jnp.float32)
    beta = jax.nn.sigmoid(bt.astype(jnp.float32))
    g = -jnp.exp(a_log.astype(jnp.float32)) * jax.nn.softplus(al.astype(jnp.float32)
                                                              + dt_bias.astype(jnp.float32))
    o, s_new = gated_delta_rule(q, k, v, g, beta, s0.astype(jnp.float32))
    o = rms_norm(o, out_gain) * jax.nn.silu(z.reshape(b, t, G_V_HEADS, G_V_DIM).astype(jnp.float32))
    return o.reshape(b, t, G_V_WIDTH).astype(h.dtype) @ w_out, new_hist, s_new


def hier_moe(h, w_group, w_router, w_gate, w_up, w_down):
    b, t, _ = h.shape
    hf = h.astype(jnp.float32)
    group_p = jax.nn.softmax(hf @ w_group.astype(jnp.float32), axis=-1)
    g_w, g_idx = lax.top_k(group_p, 1)
    fine = (hf @ w_router.astype(jnp.float32)).reshape(b, t, N_GROUPS, EXPERTS_PER_GROUP)
    fine_sel = jnp.take_along_axis(fine, g_idx[..., None], axis=2)[..., 0, :]
    top_w, top_i = lax.top_k(jax.nn.softmax(fine_sel, axis=-1), TOP_K)
    top_w = top_w / jnp.sum(top_w, axis=-1, keepdims=True)
    expert_id = g_idx * EXPERTS_PER_GROUP + top_i
    gate = jnp.sum(jax.nn.one_hot(expert_id, N_EXPERTS, dtype=jnp.float32)
                   * (g_w * top_w)[..., None], axis=-2)

    def per_seq(args):
        x_s, gate_s = args
        hid = jax.nn.silu(jnp.einsum('td,edf->tef', x_s, w_gate)) * jnp.einsum('td,edf->tef', x_s, w_up)
        hid = hid * gate_s[..., None].astype(hid.dtype)
        return jnp.einsum('tef,efd->td', hid, w_down)

    return lax.map(per_seq, (h, gate))


def setup_inputs(seed: int = 0) -> dict:
    key = jax.random.key(seed)
    keys = iter(jax.random.split(key, 40))

    def nrm(shape, scale):
        return jax.random.normal(next(keys), shape, jnp.float32) * scale

    def gain(shape):
        return 1.0 + nrm(shape, 0.01)

    dt = jnp.exp(jax.random.uniform(next(keys), (N_DELTA_LAYERS, G_V_HEADS), jnp.float32)
                 * (math.log(0.1) - math.log(0.001)) + math.log(0.001))
    return {
        'x_prompt': nrm((BATCH, SEQ, D_MODEL), 1.0),
        'x_sample': nrm((DEC_BATCH, DEC_SEQ, D_MODEL), 1.0),
        'cache_k_attn': nrm((N_ATTN_LAYERS, DEC_BATCH, PAST_LEN, A_HEADS, 2, A_QK_DIM), 1.0),
        'cache_v_attn': nrm((N_ATTN_LAYERS, DEC_BATCH, PAST_LEN, A_HEADS, A_V_DIM), 1.0),
        'state_conv_delta': nrm((N_DELTA_LAYERS, DEC_BATCH, G_CONV - 1, G_CONV_WIDTH), 1.0),
        'state_ssm_delta': nrm((N_DELTA_LAYERS, DEC_BATCH, G_V_HEADS, G_K_DIM, G_V_DIM), 0.1),
        'norm_mix': gain((DEPTH, D_MODEL)),
        'norm_ffn': gain((DEPTH, D_MODEL)),
        'attn_w_in': nrm((N_ATTN_LAYERS, D_MODEL, A_IN_WIDTH), D_MODEL ** -0.5),
        'attn_q_norm': gain((N_ATTN_LAYERS, 2, A_QK_DIM)),
        'attn_k_norm': gain((N_ATTN_LAYERS, 2, A_QK_DIM)),
        'attn_lambda': nrm((N_ATTN_LAYERS, 4, A_QK_DIM), 0.1),
        'attn_sub_norm': gain((N_ATTN_LAYERS, A_V_DIM)),
        'attn_w_out': nrm((N_ATTN_LAYERS, A_V_WIDTH, D_MODEL), A_V_WIDTH ** -0.5),
        'delta_w_in': nrm((N_DELTA_LAYERS, D_MODEL, G_IN_WIDTH), D_MODEL ** -0.5),
        'delta_conv': nrm((N_DELTA_LAYERS, G_CONV, G_CONV_WIDTH), G_CONV ** -0.5),
        'delta_a_log': jnp.log(jax.random.uniform(next(keys), (N_DELTA_LAYERS, G_V_HEADS), jnp.float32,
                                                  minval=1.0, maxval=16.0)),
        'delta_dt_bias': dt + jnp.log(-jnp.expm1(-dt)),
        'delta_out_norm': gain((N_DELTA_LAYERS, G_V_DIM)),
        'delta_w_out': nrm((N_DELTA_LAYERS, G_V_WIDTH, D_MODEL), G_V_WIDTH ** -0.5),
        'moe_w_group': nrm((DEPTH, D_MODEL, N_GROUPS), D_MODEL ** -0.5),
        'moe_w_router': nrm((DEPTH, D_MODEL, N_EXPERTS), D_MODEL ** -0.5),
        'moe_w_gate': nrm((DEPTH, N_EXPERTS, D_MODEL, D_EXPERT), D_MODEL ** -0.5),
        'moe_w_up': nrm((DEPTH, N_EXPERTS, D_MODEL, D_EXPERT), D_MODEL ** -0.5),
        'moe_w_down': nrm((DEPTH, N_EXPERTS, D_EXPERT, D_MODEL), D_EXPERT ** -0.5),
    }


def reference(x_prompt, x_sample, cache_k_attn, cache_v_attn, state_conv_delta, state_ssm_delta,
              norm_mix, norm_ffn,
              attn_w_in, attn_q_norm, attn_k_norm, attn_lambda, attn_sub_norm, attn_w_out,
              delta_w_in, delta_conv, delta_a_log, delta_dt_bias, delta_out_norm, delta_w_out,
              moe_w_group, moe_w_router, moe_w_gate, moe_w_up, moe_w_down):
    xp, xs = x_prompt, x_sample
    past_len = cache_k_attn.shape[2]
    kp_rows, vp_rows, ks_rows, vs_rows = [], [], [], []
    cp_rows, sp_rows, cs_rows, ss_rows = [], [], [], []
    for i in range(DEPTH):
        j = i // N_MIXERS
        hp = rms_norm(xp, norm_mix[i])
        hs = rms_norm(xs, norm_mix[i])
        if i % N_MIXERS == 0:
            lam_init = 0.8 - 0.6 * math.exp(-0.3 * i)
            lam = diff_lambda(attn_lambda[j], lam_init)
            qp, kp, vp = diff_attn_project(hp, attn_w_in[j], attn_q_norm[j], attn_k_norm[j])
            qs, ks, vs = diff_attn_project(hs, attn_w_in[j], attn_q_norm[j], attn_k_norm[j])
            op = diff_attention_core(qp, kp, vp, lam, 0)
            k_all = jnp.concatenate([cache_k_attn[j].astype(ks.dtype), ks], axis=1)
            v_all = jnp.concatenate([cache_v_attn[j].astype(vs.dtype), vs], axis=1)
            o_s = diff_attention_core(qs, k_all, v_all, lam, past_len)
            mp = diff_attn_output(op, attn_sub_norm[j], lam_init, attn_w_out[j])
            ms = diff_attn_output(o_s, attn_sub_norm[j], lam_init, attn_w_out[j])
            kp_rows.append(kp)
            vp_rows.append(vp)
            ks_rows.append(ks)
            vs_rows.append(vs)
        else:
            zero_hist = jnp.zeros((xp.shape[0], G_CONV - 1, G_CONV_WIDTH), xp.dtype)
            zero_state = jnp.zeros((xp.shape[0], G_V_HEADS, G_K_DIM, G_V_DIM), jnp.float32)
            mp, cpn, spn = gdn_mixer(hp, zero_hist, zero_state, delta_w_in[j], delta_conv[j],
                                     delta_a_log[j], delta_dt_bias[j], delta_out_norm[j], delta_w_out[j])
            ms, csn, ssn = gdn_mixer(hs, state_conv_delta[j], state_ssm_delta[j], delta_w_in[j],
                                     delta_conv[j], delta_a_log[j], delta_dt_bias[j],
                                     delta_out_norm[j], delta_w_out[j])
            cp_rows.append(cpn)
            sp_rows.append(spn)
            cs_rows.append(csn)
            ss_rows.append(ssn)
        xp = xp + mp
        xs = xs + ms
        xp = xp + hier_moe(rms_norm(xp, norm_ffn[i]), moe_w_group[i], moe_w_router[i],
                           moe_w_gate[i], moe_w_up[i], moe_w_down[i])
        xs = xs + hier_moe(rms_norm(xs, norm_ffn[i]), moe_w_group[i], moe_w_router[i],
                           moe_w_gate[i], moe_w_up[i], moe_w_down[i])
    return (xp, xs, jnp.stack(kp_rows), jnp.stack(vp_rows), jnp.stack(ks_rows), jnp.stack(vs_rows),
            jnp.stack(cp_rows), jnp.stack(sp_rows), jnp.stack(cs_rows), jnp.stack(ss_rows))
```

```python
import functools
import math

import jax
import jax.numpy as jnp
from jax import lax
from jax.experimental import pallas as pl
from jax.experimental.pallas import tpu as pltpu

F32 = jnp.float32
BF16 = jnp.bfloat16
EPS = 1e-6
NEG = -1e30

LANES = 128
MXU_DIM = 256
VMEM_LIMIT_BYTES = 56 * 1024 * 1024

CHUNK = 64
A_HEADS = 16
A_QK_DIM = 64
A_V_DIM = 128
G_K_HEADS = 16
G_V_HEADS = 32
G_DIM = 128
G_CONV = 4
N_GROUPS = 8
EXPERTS_PER_GROUP = 8
N_EXPERTS = 64
D_EXPERT = 512
MOE_TILE = 128


def _params(sem):
    return pltpu.CompilerParams(dimension_semantics=sem, vmem_limit_bytes=VMEM_LIMIT_BYTES)


def _mm_kernel(*refs, norm, gn, res, out2):
    it = iter(refs)
    x_ref = next(it)
    w_ref = next(it)
    gain_ref = next(it) if norm else None
    gng_ref = next(it) if gn else None
    gmat_ref = next(it) if gn else None
    res_ref = next(it) if res else None
    o_ref = next(it)
    o2_ref = next(it) if out2 else None
    h_scr = next(it) if norm else None

    if norm:
        @pl.when(pl.program_id(1) == 0)
        def _():
            x = x_ref[...]
            ms = jnp.mean(x * x, axis=-1, keepdims=True)
            h_scr[...] = (x * lax.rsqrt(ms + EPS) * gain_ref[...]).astype(BF16)
        lhs = h_scr[...]
    else:
        lhs = x_ref[...]
    y = jnp.dot(lhs, w_ref[...], preferred_element_type=F32)
    if gn:
        parts = []
        for c in range(y.shape[1] // MXU_DIM):
            yc = y[:, c * MXU_DIM:(c + 1) * MXU_DIM]
            ss = jnp.dot((yc * yc).astype(BF16), gmat_ref[...], preferred_element_type=F32)
            parts.append(yc * lax.rsqrt(ss * (1.0 / A_QK_DIM) + EPS))
        y = jnp.concatenate(parts, axis=1) * gng_ref[...]
    if res:
        y = y + res_ref[...]
    o_ref[...] = y.astype(o_ref.dtype)
    if out2:
        o2_ref[...] = y.astype(o2_ref.dtype)


def _group_ones():
    r = jnp.arange(MXU_DIM) // A_QK_DIM
    return (r[:, None] == r[None, :]).astype(BF16)


def _mm(x, w, *, gain=None, gn_gain=None, res=None, out_dtype=F32, out2_dtype=None, tm=512, tn=1024):
    M, K = x.shape
    N = w.shape[1]
    tm = min(tm, M)
    tn = min(tn, N)
    assert M % tm == 0 and N % tn == 0
    norm, gn, has_res, out2 = gain is not None, gn_gain is not None, res is not None, out2_dtype is not None
    in_specs = [pl.BlockSpec((tm, K), lambda i, j: (i, 0)), pl.BlockSpec((K, tn), lambda i, j: (0, j))]
    args = [x, w]
    if norm:
        in_specs.append(pl.BlockSpec((1, K), lambda i, j: (0, 0)))
        args.append(gain.reshape(1, K).astype(F32))
    if gn:
        assert tn % MXU_DIM == 0
        in_specs += [pl.BlockSpec((1, tn), lambda i, j: (0, j)),
                     pl.BlockSpec((MXU_DIM, MXU_DIM), lambda i, j: (0, 0))]
        args += [gn_gain.reshape(1, N).astype(F32), _group_ones()]
    if has_res:
        in_specs.append(pl.BlockSpec((tm, tn), lambda i, j: (i, j)))
        args.append(res)
    out_shape = [jax.ShapeDtypeStruct((M, N), out_dtype)]
    out_specs = [pl.BlockSpec((tm, tn), lambda i, j: (i, j))]
    if out2:
        out_shape.append(jax.ShapeDtypeStruct((M, N), out2_dtype))
        out_specs.append(pl.BlockSpec((tm, tn), lambda i, j: (i, j)))
    outs = pl.pallas_call(
        functools.partial(_mm_kernel, norm=norm, gn=gn, res=has_res, out2=out2),
        grid=(M // tm, N // tn),
        in_specs=in_specs,
        out_specs=out_specs,
        out_shape=out_shape,
        scratch_shapes=[pltpu.VMEM((tm, K), BF16)] if norm else [],
        compiler_params=_params(("parallel", "arbitrary")),
        name="mm",
    )(*args)
    return outs if out2 else outs[0]


def _softmax_update(s, v, m_prev, l_prev, acc_prev):
    m_new = jnp.maximum(m_prev, jnp.max(s, axis=-1, keepdims=True))
    p = jnp.exp(s - m_new)
    alpha = jnp.exp(m_prev - m_new)
    l_new = alpha * l_prev + jnp.sum(p, axis=-1, keepdims=True)
    acc_new = alpha * acc_prev + jnp.dot(p.astype(BF16), v, preferred_element_type=F32)
    return m_new, l_new, acc_new


def _qk(q, k):
    return lax.dot_general(q, k, (((1,), (1,)), ((), ())), preferred_element_type=F32)


def _diff_lambda(lp, lam_init):
    a = jnp.sum(lp[0:1, :] * lp[1:2, :], axis=-1, keepdims=True)
    b = jnp.sum(lp[2:3, :] * lp[3:4, :], axis=-1, keepdims=True)
    return jnp.exp(a) - jnp.exp(b) + lam_init


def _diff_finish(acc0, l0, acc1, l1, lam, sub_gain, lam_init):
    o = acc0 / l0 - lam * (acc1 / l1)
    ms = jnp.mean(o * o, axis=-1, keepdims=True)
    return o * lax.rsqrt(ms + EPS) * sub_gain * (1.0 - lam_init)


def _attn_prompt_kernel(lam_ref, sg_ref, q_ref, k_ref, v_ref, o_ref, m_scr, l_scr, acc_scr, *, lam_init, tq):
    i = pl.program_id(2)
    m_scr[...] = jnp.full(m_scr.shape, NEG, F32)
    l_scr[...] = jnp.zeros(l_scr.shape, F32)
    acc_scr[...] = jnp.zeros(acc_scr.shape, F32)
    q = q_ref[...]

    def tile(start, mask):
        k = k_ref[pl.ds(start, tq), :]
        v = v_ref[pl.ds(start, tq), :]
        for c in range(2):
            s = _qk(q[:, c * A_QK_DIM:(c + 1) * A_QK_DIM], k[:, c * A_QK_DIM:(c + 1) * A_QK_DIM])
            if mask is not None:
                s = jnp.where(mask, s, NEG)
            m, l, acc = _softmax_update(s, v, m_scr[c], l_scr[c], acc_scr[c])
            m_scr[c] = m
            l_scr[c] = l
            acc_scr[c] = acc

    def body(j, carry):
        tile(pl.multiple_of(j * tq, tq), None)
        return carry

    lax.fori_loop(0, i, body, 0)
    rq = lax.broadcasted_iota(jnp.int32, (tq, tq), 0) // CHUNK
    ck = lax.broadcasted_iota(jnp.int32, (tq, tq), 1) // CHUNK
    tile(pl.multiple_of(i * tq, tq), ck <= rq)
    lam = _diff_lambda(lam_ref[...], lam_init)
    o_ref[...] = _diff_finish(acc_scr[0], l_scr[0], acc_scr[1], l_scr[1], lam, sg_ref[...], lam_init).astype(BF16)


def _attn_prompt(q, k, v, lam_p, sub_gain, lam_init, batch, seq, tq=512):
    tq = min(tq, seq)
    nq = seq // tq
    width = A_HEADS * A_V_DIM
    return pl.pallas_call(
        functools.partial(_attn_prompt_kernel, lam_init=lam_init, tq=tq),
        grid=(batch, A_HEADS, nq),
        in_specs=[
            pl.BlockSpec((4, A_QK_DIM), lambda b, h, i: (0, 0)),
            pl.BlockSpec((1, A_V_DIM), lambda b, h, i: (0, 0)),
            pl.BlockSpec((tq, A_V_DIM), lambda b, h, i: (b * nq + i, h)),
            pl.BlockSpec((seq, A_V_DIM), lambda b, h, i: (b, h)),
            pl.BlockSpec((seq, A_V_DIM), lambda b, h, i: (b, h)),
        ],
        out_specs=pl.BlockSpec((tq, A_V_DIM), lambda b, h, i: (b * nq + i, h)),
        out_shape=jax.ShapeDtypeStruct((batch * seq, width), BF16),
        scratch_shapes=[pltpu.VMEM((2, tq, 1), F32), pltpu.VMEM((2, tq, 1), F32),
                        pltpu.VMEM((2, tq, A_V_DIM), F32)],
        compiler_params=_params(("parallel", "parallel", "arbitrary")),
        name="attn_prompt",
    )(lam_p.astype(F32), sub_gain.reshape(1, A_V_DIM).astype(F32), q, k, v)


def _attn_sample_kernel(lam_ref, sg_ref, q_ref, kn_ref, vn_ref, ck_ref, cv_ref, o_ref, m_scr, l_scr, acc_scr,
                        *, lam_init, nk):
    j = pl.program_id(1)

    @pl.when(j == 0)
    def _():
        m_scr[...] = jnp.full(m_scr.shape, NEG, F32)
        l_scr[...] = jnp.zeros(l_scr.shape, F32)
        acc_scr[...] = jnp.zeros(acc_scr.shape, F32)

    def update(load_k, load_v):
        for h in range(A_HEADS):
            v = load_v(h)
            for c in range(2):
                idx = 2 * h + c
                lo = h * A_V_DIM + c * A_QK_DIM
                s = _qk(q_ref[:, lo:lo + A_QK_DIM], load_k(lo))
                cols = slice(h * A_V_DIM, (h + 1) * A_V_DIM)
                m, l, acc = _softmax_update(s, v, m_scr[idx], l_scr[idx], acc_scr[c, :, cols])
                m_scr[idx] = m
                l_scr[idx] = l
                acc_scr[c, :, cols] = acc

    update(lambda lo: ck_ref[0, :, lo:lo + A_QK_DIM].astype(BF16),
           lambda h: cv_ref[0, :, h * A_V_DIM:(h + 1) * A_V_DIM].astype(BF16))

    @pl.when(j == nk - 1)
    def _():
        update(lambda lo: kn_ref[:, lo:lo + A_QK_DIM],
               lambda h: vn_ref[:, h * A_V_DIM:(h + 1) * A_V_DIM])
        lam = _diff_lambda(lam_ref[...], lam_init)
        for h in range(A_HEADS):
            cols = slice(h * A_V_DIM, (h + 1) * A_V_DIM)
            o = _diff_finish(acc_scr[0, :, cols], l_scr[2 * h], acc_scr[1, :, cols], l_scr[2 * h + 1],
                             lam, sg_ref[...], lam_init)
            o_ref[:, cols] = o.astype(BF16)


def _attn_sample(q, k_new, v_new, cache_k, cache_v, lam_p, sub_gain, lam_init, batch, seq, tk=512):
    past = cache_k.shape[1]
    assert seq == CHUNK and past % CHUNK == 0
    tk = min(tk, past)
    nk = past // tk
    width = A_HEADS * A_V_DIM
    return pl.pallas_call(
        functools.partial(_attn_sample_kernel, lam_init=lam_init, nk=nk),
        grid=(batch, nk),
        in_specs=[
            pl.BlockSpec((4, A_QK_DIM), lambda b, j: (0, 0)),
            pl.BlockSpec((1, A_V_DIM), lambda b, j: (0, 0)),
            pl.BlockSpec((seq, width), lambda b, j: (b, 0)),
            pl.BlockSpec((seq, width), lambda b, j: (b, 0)),
            pl.BlockSpec((seq, width), lambda b, j: (b, 0)),
            pl.BlockSpec((1, tk, width), lambda b, j: (b, j, 0)),
            pl.BlockSpec((1, tk, width), lambda b, j: (b, j, 0)),
        ],
        out_specs=pl.BlockSpec((seq, width), lambda b, j: (b, 0)),
        out_shape=jax.ShapeDtypeStruct((batch * seq, width), BF16),
        scratch_shapes=[pltpu.VMEM((2 * A_HEADS, seq, 1), F32), pltpu.VMEM((2 * A_HEADS, seq, 1), F32),
                        pltpu.VMEM((2, seq, width), F32)],
        compiler_params=_params(("parallel", "arbitrary")),
        name="attn_sample",
    )(lam_p.astype(F32), sub_gain.reshape(1, A_V_DIM).astype(F32), q, k_new, v_new, cache_k, cache_v)


def _router_kernel(x_ref, gain_ref, w_ref, h_ref, ids_ref, gates_ref):
    x = x_ref[...]
    ms = jnp.mean(x * x, axis=-1, keepdims=True)
    h = x * lax.rsqrt(ms + EPS) * gain_ref[...]
    h_ref[...] = h.astype(BF16)
    logits = jnp.dot(h, w_ref[...], preferred_element_type=F32, precision=lax.Precision.HIGHEST)
    lane = lax.broadcasted_iota(jnp.int32, logits.shape, 1)
    gl = jnp.where(lane < N_GROUPS, logits, NEG)
    gm = jnp.max(gl, axis=-1, keepdims=True)
    g_w = 1.0 / jnp.sum(jnp.exp(gl - gm), axis=-1, keepdims=True)
    g_idx = jnp.min(jnp.where(gl == gm, lane, LANES), axis=-1, keepdims=True)
    first = N_GROUPS + g_idx * EXPERTS_PER_GROUP
    fl = jnp.where((lane >= first) & (lane < first + EXPERTS_PER_GROUP), logits, NEG)
    m1 = jnp.max(fl, axis=-1, keepdims=True)
    i1 = jnp.min(jnp.where(fl == m1, lane, LANES), axis=-1, keepdims=True)
    fl2 = jnp.where(lane == i1, NEG, fl)
    m2 = jnp.max(fl2, axis=-1, keepdims=True)
    i2 = jnp.min(jnp.where(fl2 == m2, lane, LANES), axis=-1, keepdims=True)
    e2 = jnp.exp(m2 - m1)
    w1 = 1.0 / (1.0 + e2)
    ids_ref[...] = jnp.where(lane == 0, i1 - N_GROUPS, jnp.where(lane == 1, i2 - N_GROUPS, 0))
    gates_ref[...] = jnp.where(lane == 0, g_w * w1, jnp.where(lane == 1, g_w * (e2 * w1), 0.0))


def _router(x, gain, w_rt, tm=512):
    M, D = x.shape
    tm = min(tm, M)
    return pl.pallas_call(
        _router_kernel,
        grid=(M // tm,),
        in_specs=[pl.BlockSpec((tm, D), lambda i: (i, 0)), pl.BlockSpec((1, D), lambda i: (0, 0)),
                  pl.BlockSpec((D, LANES), lambda i: (0, 0))],
        out_specs=[pl.BlockSpec((tm, D), lambda i: (i, 0)), pl.BlockSpec((tm, LANES), lambda i: (i, 0)),
                   pl.BlockSpec((tm, LANES), lambda i: (i, 0))],
        out_shape=[jax.ShapeDtypeStruct((M, D), BF16), jax.ShapeDtypeStruct((M, LANES), jnp.int32),
                   jax.ShapeDtypeStruct((M, LANES), F32)],
        compiler_params=_params(("parallel",)),
        name="moe_router",
    )(x, gain.reshape(1, D).astype(F32), w_rt)


def _experts_kernel(te_ref, nu_ref, x_ref, wg_ref, wu_ref, wd_ref, o_ref, wgu_scr, wd_scr):
    i = pl.program_id(0)
    prev = te_ref[jnp.maximum(i - 1, 0)]

    @pl.when((i == 0) | (te_ref[i] != prev))
    def _():
        wgu_scr[:, :D_EXPERT] = wg_ref[0].astype(BF16)
        wgu_scr[:, D_EXPERT:] = wu_ref[0].astype(BF16)
        wd_scr[...] = wd_ref[0].astype(BF16)

    @pl.when(i < nu_ref[0])
    def _():
        gu = jnp.dot(x_ref[...], wgu_scr[...], preferred_element_type=F32)
        g = gu[:, :D_EXPERT]
        hid = g * jax.nn.sigmoid(g) * gu[:, D_EXPERT:]
        o_ref[...] = jnp.dot(hid.astype(BF16), wd_scr[...], preferred_element_type=F32)

    @pl.when(i >= nu_ref[0])
    def _():
        o_ref[...] = jnp.zeros(o_ref.shape, F32)


def _experts(xs, tile_expert, n_used, w_gate, w_up, w_down):
    P, D = xs.shape
    n_tiles = P // MOE_TILE
    grid_spec = pltpu.PrefetchScalarGridSpec(
        num_scalar_prefetch=2,
        grid=(n_tiles,),
        in_specs=[
            pl.BlockSpec((MOE_TILE, D), lambda i, te, nu: (i, 0)),
            pl.BlockSpec((1, D, D_EXPERT), lambda i, te, nu: (te[i], 0, 0)),
            pl.BlockSpec((1, D, D_EXPERT), lambda i, te, nu: (te[i], 0, 0)),
            pl.BlockSpec((1, D_EXPERT, D), lambda i, te, nu: (te[i], 0, 0)),
        ],
        out_specs=pl.BlockSpec((MOE_TILE, D), lambda i, te, nu: (i, 0)),
        scratch_shapes=[pltpu.VMEM((D, 2 * D_EXPERT), BF16), pltpu.VMEM((D_EXPERT, D), BF16)],
    )
    return pl.pallas_call(
        _experts_kernel,
        grid_spec=grid_spec,
        out_shape=jax.ShapeDtypeStruct((P, D), F32),
        compiler_params=_params(("arbitrary",)),
        name="moe_experts",
    )(tile_expert, n_used, xs, w_gate, w_up, w_down)


def _combine_kernel(x_ref, y0_ref, y1_ref, g_ref, o_ref):
    g = g_ref[...]
    o_ref[...] = x_ref[...] + g[:, 0:1] * y0_ref[...] + g[:, 1:2] * y1_ref[...]


def _combine(x, y0, y1, gates, tm=512):
    M, D = x.shape
    tm = min(tm, M)
    row = pl.BlockSpec((tm, D), lambda i: (i, 0))
    return pl.pallas_call(
        _combine_kernel,
        grid=(M // tm,),
        in_specs=[row, row, row, pl.BlockSpec((tm, LANES), lambda i: (i, 0))],
        out_specs=row,
        out_shape=jax.ShapeDtypeStruct((M, D), F32),
        compiler_params=_params(("parallel",)),
        name="moe_combine",
    )(x, y0, y1, gates)


def _route_plan(ids):
    flat = ids.reshape(-1)
    n_assign = flat.shape[0]
    order = jnp.argsort(flat, stable=True).astype(jnp.int32)
    counts = jnp.bincount(flat, length=N_EXPERTS).astype(jnp.int32)
    starts = jnp.cumsum(counts) - counts
    padded = (counts + MOE_TILE - 1) // MOE_TILE * MOE_TILE
    pends = jnp.cumsum(padded)
    pstarts = pends - padded
    sorted_e = flat[order]
    dest = pstarts[sorted_e] + jnp.arange(n_assign, dtype=jnp.int32) - starts[sorted_e]
    n_rows = n_assign + N_EXPERTS * MOE_TILE
    row_token = jnp.zeros((n_rows,), jnp.int32).at[dest].set(order // 2)
    pos = jnp.zeros((n_assign,), jnp.int32).at[order].set(dest).reshape(-1, 2)
    n_tiles = n_rows // MOE_TILE
    n_used = (pends[-1] // MOE_TILE).astype(jnp.int32)
    tile_start = jnp.arange(n_tiles, dtype=jnp.int32) * MOE_TILE
    tile_e = jnp.searchsorted(pends, tile_start, side="right").astype(jnp.int32)
    last_e = tile_e[jnp.maximum(n_used - 1, 0)]
    tile_e = jnp.where(jnp.arange(n_tiles) < n_used, tile_e, last_e)
    return row_token, pos, tile_e, n_used.reshape(1)


def _moe(x, gain, w_group, w_router, w_gate, w_up, w_down):
    M, D = x.shape
    pad = jnp.zeros((D, LANES - N_GROUPS - N_EXPERTS), F32)
    w_rt = jnp.concatenate([w_group.astype(F32), w_router.astype(F32), pad], axis=1)
    h, ids, gates = _router(x, gain, w_rt)
    row_token, pos, tile_e, n_used = _route_plan(ids[:, :2])
    xs = jnp.take(h, row_token, axis=0)
    y = _experts(xs, tile_e, n_used, w_gate, w_up, w_down)
    return _combine(x, jnp.take(y, pos[:, 0], axis=0), jnp.take(y, pos[:, 1], axis=0), gates)


def _conv_kernel(u_ref, prev_ref, hist_ref, w_ref, o_ref, ext, *, tt, qk_blocks, q_blocks):
    t = pl.program_id(1)
    c = pl.program_id(2)
    ext[8:, :] = u_ref[0]

    @pl.when(t == 0)
    def _():
        ext[8 - (G_CONV - 1):8, :] = hist_ref[0]

    @pl.when(t > 0)
    def _():
        ext[0:8, :] = prev_ref[0]

    w = w_ref[...]
    y = ext[8:8 + tt, :] * w[3:4, :]
    for j in range(G_CONV - 1):
        y = y + ext[5 + j:5 + j + tt, :] * w[j:j + 1, :]
    y = y * jax.nn.sigmoid(y)

    @pl.when(c >= qk_blocks)
    def _():
        o_ref[0] = y

    @pl.when(c < qk_blocks)
    def _():
        scale = jnp.where(c < q_blocks, G_DIM ** -0.5, 1.0)
        for s in range(y.shape[1] // G_DIM):
            blk = y[:, s * G_DIM:(s + 1) * G_DIM]
            ss = jnp.sum(blk * blk, axis=-1, keepdims=True)
            o_ref[0, :, s * G_DIM:(s + 1) * G_DIM] = blk * (lax.rsqrt(ss + EPS) * scale)


def _conv(u, hist, w, tt=512, tc=512):
    B, T, C = u.shape
    tt = min(tt, T)
    assert T % tt == 0 and tt % 8 == 0 and C % tc == 0
    qk_width = 2 * G_K_HEADS * G_DIM
    rb = tt // 8
    return pl.pallas_call(
        functools.partial(_conv_kernel, tt=tt, qk_blocks=qk_width // tc, q_blocks=qk_width // 2 // tc),
        grid=(B, T // tt, C // tc),
        in_specs=[
            pl.BlockSpec((1, tt, tc), lambda b, t, c: (b, t, c)),
            pl.BlockSpec((1, 8, tc), lambda b, t, c: (b, jnp.maximum(t * rb - 1, 0), c)),
            pl.BlockSpec((1, G_CONV - 1, tc), lambda b, t, c: (b, 0, c)),
            pl.BlockSpec((G_CONV, tc), lambda b, t, c: (0, c)),
        ],
        out_specs=pl.BlockSpec((1, tt, tc), lambda b, t, c: (b, t, c)),
        out_shape=jax.ShapeDtypeStruct((B, T, C), F32),
        scratch_shapes=[pltpu.VMEM((tt + 8, tc), F32)],
        compiler_params=_params(("parallel", "parallel", "parallel")),
        name="gdn_conv",
    )(u, u, hist, w)


def _dot32(a, b):
    return jnp.dot(a, b, preferred_element_type=F32, precision=lax.Precision.HIGHEST)


def _dotb(a, b):
    return jnp.dot(a.astype(BF16), b.astype(BF16), preferred_element_type=F32)


def _delta_kernel(q_ref, k_ref, v_ref, z_ref, bt_ref, al_ref, alog_ref, dtb_ref, og_ref, s0_ref,
                  o_ref, s_ref):
    c = pl.program_id(2)

    @pl.when(c == 0)
    def _():
        s_ref[...] = s0_ref[...]

    L = CHUNK
    ri = lax.broadcasted_iota(jnp.int32, (L, L), 0)
    ci = lax.broadcasted_iota(jnp.int32, (L, L), 1)
    eye = ri == ci
    incl = ri >= ci
    strict = ri > ci

    def to_col(row):
        return jnp.sum(jnp.where(eye, row, 0.0), axis=1, keepdims=True)

    q = q_ref[0]
    k = k_ref[0]
    kb = k.astype(BF16)
    kk = _qk(kb, kb)
    qk = _qk(q.astype(BF16), kb)
    for j in range(2):
        beta_row = jax.nn.sigmoid(bt_ref[0, j, 0])
        x = al_ref[0, j, 0] + dtb_ref[0, j:j + 1, :]
        softplus = jnp.maximum(x, 0.0) + jnp.log(1.0 + jnp.exp(-jnp.abs(x)))
        g_row = -jnp.exp(alog_ref[0, j:j + 1, :]) * softplus
        g_col = to_col(g_row)
        gc_row = jnp.sum(jnp.where(ri <= ci, g_col, 0.0), axis=0, keepdims=True)
        gc_col = to_col(gc_row)
        beta_col = to_col(beta_row)
        decay = jnp.where(incl, jnp.exp(jnp.where(incl, gc_col - gc_row, 0.0)), 0.0)
        a_mat = jnp.where(strict, beta_col * kk * decay, 0.0)
        n_pow = -a_mat
        t_inv = jnp.where(eye, 1.0, 0.0) + n_pow
        for _ in range(5):
            n_pow = _dot32(n_pow, n_pow)
            t_inv = t_inv + _dot32(t_inv, n_pow)
        v = v_ref[0, :, j * G_DIM:(j + 1) * G_DIM]
        exp_gc = jnp.exp(gc_col)
        u = _dot32(t_inv, v * beta_col)
        w = _dot32(t_inv, k * (beta_col * exp_gc))
        gc_last = gc_row[:, L - 1:L]
        q_dec = q * exp_gc
        k_dec = k * jnp.exp(gc_last - gc_col)
        s = s_ref[0, j]
        v_new = u - _dotb(w, s)
        o = _dotb(q_dec, s) + _dotb(qk * decay, v_new)
        s_ref[0, j] = s * jnp.exp(gc_last) + _dotb(k_dec.T, v_new)
        ms = jnp.mean(o * o, axis=-1, keepdims=True)
        z = z_ref[0, :, j * G_DIM:(j + 1) * G_DIM].astype(F32)
        o = o * lax.rsqrt(ms + EPS) * og_ref[...] * (z * jax.nn.sigmoid(z))
        o_ref[0, :, j * G_DIM:(j + 1) * G_DIM] = o.astype(BF16)


def _delta(cqkv, z, bt_al, a_log, dt_bias, out_gain, s0):
    B, T, _ = cqkv.shape
    n = T // CHUNK
    rep = G_V_HEADS // G_K_HEADS
    kq = G_K_HEADS
    vw = rep * G_DIM
    per_head = lambda p: jnp.broadcast_to(p.astype(F32).reshape(kq, rep, 1), (kq, rep, CHUNK))
    gate_spec = lambda off: pl.BlockSpec((1, rep, 1, 1, CHUNK), lambda b, h, c: (b, off + h, c, 0, 0))
    head_spec = pl.BlockSpec((1, rep, CHUNK), lambda b, h, c: (h, 0, 0))
    state_spec = pl.BlockSpec((1, rep, G_DIM, G_DIM), lambda b, h, c: (b, h, 0, 0))
    return pl.pallas_call(
        _delta_kernel,
        grid=(B, kq, n),
        in_specs=[
            pl.BlockSpec((1, CHUNK, G_DIM), lambda b, h, c: (b, c, h)),
            pl.BlockSpec((1, CHUNK, G_DIM), lambda b, h, c: (b, c, kq + h)),
            pl.BlockSpec((1, CHUNK, vw), lambda b, h, c: (b, c, 2 * kq * G_DIM // vw + h)),
            pl.BlockSpec((1, CHUNK, vw), lambda b, h, c: (b, c, h)),
            gate_spec(0), gate_spec(kq),
            head_spec, head_spec,
            pl.BlockSpec((1, G_DIM), lambda b, h, c: (0, 0)),
            state_spec,
        ],
        out_specs=[pl.BlockSpec((1, CHUNK, vw), lambda b, h, c: (b, c, h)), state_spec],
        out_shape=[jax.ShapeDtypeStruct((B, T, G_V_HEADS * G_DIM), BF16),
                   jax.ShapeDtypeStruct(s0.shape, F32)],
        compiler_params=_params(("parallel", "parallel", "arbitrary")),
        name="gdn_delta",
    )(cqkv, cqkv, cqkv, z, bt_al, bt_al, per_head(a_log), per_head(dt_bias),
      out_gain.reshape(1, G_DIM).astype(F32), s0)


def _gdn(x, groups, conv_hists, states, gain, w_in, conv_w, a_log, dt_bias, out_gain, w_out):
    M, D = x.shape
    qkv_w = 2 * G_K_HEADS * G_DIM + G_V_HEADS * G_DIM
    v_w = G_V_HEADS * G_DIM
    w_qkv = w_in[:, :qkv_w].astype(BF16)
    w_z = w_in[:, qkv_w:qkv_w + v_w].astype(BF16)
    w_ba = jnp.pad(w_in[:, qkv_w + v_w:], ((0, 0), (0, LANES - 2 * G_V_HEADS))).astype(BF16)
    u = _mm(x, w_qkv, gain=gain)
    z = _mm(x, w_z, gain=gain, out_dtype=BF16)
    ba = _mm(x, w_ba, gain=gain)
    outs, tails, new_states = [], [], []
    for (row0, B, T), hist, s0 in zip(groups, conv_hists, states):
        n = T // CHUNK
        ug = u[row0:row0 + B * T].reshape(B, T, qkv_w)
        zg = z[row0:row0 + B * T].reshape(B, T, v_w)
        bag = ba[row0:row0 + B * T, :2 * G_V_HEADS].reshape(B, n, CHUNK, 2 * G_V_HEADS)
        bag = jnp.transpose(bag, (0, 3, 1, 2)).reshape(B, 2 * G_V_HEADS, n, 1, CHUNK)
        cqkv = _conv(ug, hist, conv_w.astype(F32))
        o, s_new = _delta(cqkv, zg, bag, a_log, dt_bias, out_gain, s0)
        outs.append(o.reshape(B * T, v_w))
        assert T >= G_CONV - 1
        tails.append(ug[:, T - (G_CONV - 1):])
        new_states.append(s_new)
    o_all = jnp.concatenate(outs, axis=0)
    return _mm(o_all, w_out.astype(BF16), res=x), tails, new_states


def _diff_attn(x, prompt, sample, cache_k, cache_v, gain, w_in, q_gain, k_gain, lam_p, sub_gain, w_out,
               lam_init):
    qk_w = A_HEADS * 2 * A_QK_DIM
    scale = A_QK_DIM ** -0.5
    tile = lambda g: jnp.tile(g.astype(F32).reshape(-1), A_HEADS)
    q = _mm(x, w_in[:, :qk_w].astype(BF16), gain=gain, gn_gain=tile(q_gain) * scale, out_dtype=BF16)
    k, kb = _mm(x, w_in[:, qk_w:2 * qk_w].astype(BF16), gain=gain, gn_gain=tile(k_gain), out2_dtype=BF16)
    v, vb = _mm(x, w_in[:, 2 * qk_w:].astype(BF16), gain=gain, out2_dtype=BF16)
    (bp, tp), (bs, ts) = prompt, sample
    mp = bp * tp
    op = _attn_prompt(q[:mp], kb[:mp], vb[:mp], lam_p, sub_gain, lam_init, bp, tp)
    past = cache_k.shape[1]
    o_s = _attn_sample(q[mp:], kb[mp:], vb[mp:], cache_k.reshape(bs, past, qk_w),
                       cache_v.reshape(bs, past, A_HEADS * A_V_DIM), lam_p, sub_gain, lam_init, bs, ts)
    o = jnp.concatenate([op, o_s], axis=0)
    return _mm(o, w_out.astype(BF16), res=x), k, v


def kernel(x_prompt, x_sample, cache_k_attn, cache_v_attn, state_conv_delta, state_ssm_delta, norm_mix, norm_ffn, attn_w_in, attn_q_norm, attn_k_norm, attn_lambda, attn_sub_norm, attn_w_out, delta_w_in, delta_conv, delta_a_log, delta_dt_bias, delta_out_norm, delta_w_out, moe_w_group, moe_w_router, moe_w_gate, moe_w_up, moe_w_down):
    bp, tp, d = x_prompt.shape
    bs, ts, _ = x_sample.shape
    mp = bp * tp
    x = jnp.concatenate([x_prompt.reshape(mp, d), x_sample.reshape(bs * ts, d)], axis=0).astype(F32)

    def moe(x, i):
        return _moe(x, norm_ffn[i], moe_w_group[i], moe_w_router[i], moe_w_gate[i], moe_w_up[i], moe_w_down[i])

    lam_init = 0.8 - 0.6 * math.exp(-0.3 * 0)
    x, k, v = _diff_attn(x, (bp, tp), (bs, ts), cache_k_attn[0], cache_v_attn[0], norm_mix[0], attn_w_in[0],
                         attn_q_norm[0], attn_k_norm[0], attn_lambda[0], attn_sub_norm[0], attn_w_out[0], lam_init)
    x = moe(x, 0)
    hists = [jnp.zeros((bp, G_CONV - 1, state_conv_delta.shape[-1]), F32), state_conv_delta[0].astype(F32)]
    states = [jnp.zeros((bp,) + state_ssm_delta.shape[2:], F32), state_ssm_delta[0].astype(F32)]
    x, tails, new_states = _gdn(x, [(0, bp, tp), (mp, bs, ts)], hists, states, norm_mix[1], delta_w_in[0],
                                delta_conv[0], delta_a_log[0], delta_dt_bias[0], delta_out_norm[0], delta_w_out[0])
    x = moe(x, 1)

    kv_p = lambda a, last: a[:mp].reshape((1, bp, tp, A_HEADS) + last)
    kv_s = lambda a, last: a[mp:].reshape((1, bs, ts, A_HEADS) + last)
    return (x[:mp].reshape(bp, tp, d), x[mp:].reshape(bs, ts, d),
            kv_p(k, (2, A_QK_DIM)), kv_p(v, (A_V_DIM,)), kv_s(k, (2, A_QK_DIM)), kv_s(v, (A_V_DIM,)),
            tails[0][None], new_states[0][None], tails[1][None], new_states[1][None])
```

```python
import functools
import math

import jax
import jax.numpy as jnp
from jax import lax
from jax.experimental import pallas as pl
from jax.experimental.pallas import tpu as pltpu

F32 = jnp.float32
BF16 = jnp.bfloat16
EPS = 1e-6
NEG = -1e30
LOG2E = 1.4426950408889634

LANES = 128
MXU_DIM = 256
VMEM_LIMIT_BYTES = 56 * 1024 * 1024

CHUNK = 64
A_HEADS = 16
A_QK_DIM = 64
A_V_DIM = 128
G_K_HEADS = 16
G_V_HEADS = 32
G_DIM = 128
G_CONV = 4
G_SUB = 16
G_PACK = 4
ROW_DMA_UNROLL = 8
N_GROUPS = 8
EXPERTS_PER_GROUP = 8
N_EXPERTS = 64
D_EXPERT = 512
MOE_TILE = 256


def _params(sem):
    return pltpu.CompilerParams(dimension_semantics=sem, vmem_limit_bytes=VMEM_LIMIT_BYTES)


def _mm_kernel(*refs, norm, gn, res, out2):
    it = iter(refs)
    x_ref = next(it)
    w_ref = next(it)
    gain_ref = next(it) if norm else None
    gng_ref = next(it) if gn else None
    gmat_ref = next(it) if gn else None
    res_ref = next(it) if res else None
    o_ref = next(it)
    o2_ref = next(it) if out2 else None
    h_scr = next(it) if norm else None

    if norm:
        @pl.when(pl.program_id(1) == 0)
        def _():
            x = x_ref[...]
            ms = jnp.mean(x * x, axis=-1, keepdims=True)
            h_scr[...] = (x * lax.rsqrt(ms + EPS) * gain_ref[...]).astype(BF16)
        lhs = h_scr[...]
    else:
        lhs = x_ref[...]
    y = jnp.dot(lhs, w_ref[...], preferred_element_type=F32)
    if gn:
        parts = []
        for c in range(y.shape[1] // MXU_DIM):
            yc = y[:, c * MXU_DIM:(c + 1) * MXU_DIM]
            ss = jnp.dot((yc * yc).astype(BF16), gmat_ref[...], preferred_element_type=F32)
            parts.append(yc * lax.rsqrt(ss * (1.0 / A_QK_DIM) + EPS))
        y = jnp.concatenate(parts, axis=1) * gng_ref[...]
    if res:
        y = y + res_ref[...]
    o_ref[...] = y.astype(o_ref.dtype)
    if out2:
        o2_ref[...] = y.astype(o2_ref.dtype)


def _group_ones():
    r = jnp.arange(MXU_DIM) // A_QK_DIM
    return (r[:, None] == r[None, :]).astype(BF16)


def _mm(x, w, *, gain=None, gn_gain=None, res=None, out_dtype=F32, out2_dtype=None, tm=512, tn=1024):
    M, K = x.shape
    N = w.shape[1]
    tm = min(tm, M)
    tn = min(tn, N)
    assert M % tm == 0 and N % tn == 0
    norm, gn, has_res, out2 = gain is not None, gn_gain is not None, res is not None, out2_dtype is not None
    in_specs = [pl.BlockSpec((tm, K), lambda i, j: (i, 0)), pl.BlockSpec((K, tn), lambda i, j: (0, j))]
    args = [x, w]
    if norm:
        in_specs.append(pl.BlockSpec((1, K), lambda i, j: (0, 0)))
        args.append(gain.reshape(1, K).astype(F32))
    if gn:
        assert tn % MXU_DIM == 0
        in_specs += [pl.BlockSpec((1, tn), lambda i, j: (0, j)),
                     pl.BlockSpec((MXU_DIM, MXU_DIM), lambda i, j: (0, 0))]
        args += [gn_gain.reshape(1, N).astype(F32), _group_ones()]
    if has_res:
        in_specs.append(pl.BlockSpec((tm, tn), lambda i, j: (i, j)))
        args.append(res)
    out_shape = [jax.ShapeDtypeStruct((M, N), out_dtype)]
    out_specs = [pl.BlockSpec((tm, tn), lambda i, j: (i, j))]
    if out2:
        out_shape.append(jax.ShapeDtypeStruct((M, N), out2_dtype))
        out_specs.append(pl.BlockSpec((tm, tn), lambda i, j: (i, j)))
    outs = pl.pallas_call(
        functools.partial(_mm_kernel, norm=norm, gn=gn, res=has_res, out2=out2),
        grid=(M // tm, N // tn),
        in_specs=in_specs,
        out_specs=out_specs,
        out_shape=out_shape,
        scratch_shapes=[pltpu.VMEM((tm, K), BF16)] if norm else [],
        compiler_params=_params(("parallel", "arbitrary")),
        name="mm",
    )(*args)
    return outs if out2 else outs[0]


def _lanes(x, width):
    if width < LANES:
        return x[:, :width]
    assert width % LANES == 0
    return jnp.concatenate([x] * (width // LANES), axis=1)


def _softmax_update(s, v, m_prev, l_prev, acc_prev):
    m_new = jnp.maximum(m_prev, jnp.max(s, axis=-1, keepdims=True))
    p = jnp.exp2(s - _lanes(m_new, s.shape[1]))
    alpha = jnp.exp2(m_prev - m_new)
    l_new = alpha * l_prev + jnp.sum(p, axis=-1, keepdims=True)
    acc_new = alpha * acc_prev + jnp.dot(p.astype(BF16), v, preferred_element_type=F32)
    return m_new, l_new, acc_new


def _qk(q, k):
    return lax.dot_general(q, k, (((1,), (1,)), ((), ())), preferred_element_type=F32)


def _map_stack(q):
    lane = lax.broadcasted_iota(jnp.int32, q.shape, 1)
    zero = jnp.zeros_like(q)
    return jnp.concatenate([jnp.where(lane < A_QK_DIM, q, zero), jnp.where(lane >= A_QK_DIM, q, zero)], axis=0)


def _diff_lambda(lp, lam_init):
    a = jnp.sum(lp[0:1, :] * lp[1:2, :], axis=-1, keepdims=True)
    b = jnp.sum(lp[2:3, :] * lp[3:4, :], axis=-1, keepdims=True)
    return jnp.exp(a) - jnp.exp(b) + lam_init


def _diff_finish(acc, l, lam, sub_gain, lam_init):
    rows = acc.shape[0] // 2
    o = acc[:rows] / l[:rows] - lam * (acc[rows:] / l[rows:])
    ms = jnp.mean(o * o, axis=-1, keepdims=True)
    return o * lax.rsqrt(ms + EPS) * sub_gain * (1.0 - lam_init)


def _attn_prompt_kernel(lam_ref, sg_ref, bias_ref, q_ref, k_ref, v_ref, o_ref, m_scr, l_scr, acc_scr,
                        *, lam_init, tq):
    i = pl.program_id(2)
    m_scr[...] = jnp.full(m_scr.shape, NEG, F32)
    l_scr[...] = jnp.zeros(l_scr.shape, F32)
    acc_scr[...] = jnp.zeros(acc_scr.shape, F32)
    qs = _map_stack(q_ref[...])

    def tile(start, bias):
        s = _qk(qs, k_ref[pl.ds(start, tq), :])
        if bias is not None:
            s = s + bias
        m, l, acc = _softmax_update(s, v_ref[pl.ds(start, tq), :], m_scr[...], l_scr[...], acc_scr[...])
        m_scr[...] = m
        l_scr[...] = l
        acc_scr[...] = acc

    def body(j, carry):
        tile(pl.multiple_of(j * tq, tq), None)
        return carry

    lax.fori_loop(0, i, body, 0)
    tile(pl.multiple_of(i * tq, tq), bias_ref[...])
    lam = _diff_lambda(lam_ref[...], lam_init)
    o_ref[...] = _diff_finish(acc_scr[...], l_scr[...], lam, sg_ref[...], lam_init).astype(BF16)


def _attn_prompt(q, k, v, lam_p, sub_gain, lam_init, batch, seq, tq=512):
    tq = min(tq, seq)
    nq = seq // tq
    width = A_HEADS * A_V_DIM
    chunk = jnp.arange(tq) // CHUNK
    bias = jnp.where(chunk[None, :] <= chunk[:, None], 0.0, NEG).astype(F32)
    bias = jnp.concatenate([bias, bias], axis=0)
    return pl.pallas_call(
        functools.partial(_attn_prompt_kernel, lam_init=lam_init, tq=tq),
        grid=(batch, A_HEADS, nq),
        in_specs=[
            pl.BlockSpec((4, A_QK_DIM), lambda b, h, i: (0, 0)),
            pl.BlockSpec((1, A_V_DIM), lambda b, h, i: (0, 0)),
            pl.BlockSpec((2 * tq, tq), lambda b, h, i: (0, 0)),
            pl.BlockSpec((tq, A_V_DIM), lambda b, h, i: (b * nq + i, h)),
            pl.BlockSpec((seq, A_V_DIM), lambda b, h, i: (b, h)),
            pl.BlockSpec((seq, A_V_DIM), lambda b, h, i: (b, h)),
        ],
        out_specs=pl.BlockSpec((tq, A_V_DIM), lambda b, h, i: (b * nq + i, h)),
        out_shape=jax.ShapeDtypeStruct((batch * seq, width), BF16),
        scratch_shapes=[pltpu.VMEM((2 * tq, LANES), F32), pltpu.VMEM((2 * tq, LANES), F32),
                        pltpu.VMEM((2 * tq, A_V_DIM), F32)],
        compiler_params=_params(("parallel", "parallel", "arbitrary")),
        name="attn_prompt",
    )(lam_p.astype(F32), sub_gain.reshape(1, A_V_DIM).astype(F32), bias, q, k, v)


def _attn_sample_kernel(lam_ref, sg_ref, q_ref, kn_ref, vn_ref, ck_ref, cv_ref, o_ref, m_scr, l_scr, acc_scr,
                        *, lam_init, nk):
    j = pl.program_id(1)

    @pl.when(j == 0)
    def _():
        m_scr[...] = jnp.full(m_scr.shape, NEG, F32)
        l_scr[...] = jnp.zeros(l_scr.shape, F32)
        acc_scr[...] = jnp.zeros(acc_scr.shape, F32)

    def update(load_k, load_v):
        for h in range(A_HEADS):
            cols = slice(h * A_V_DIM, (h + 1) * A_V_DIM)
            s = _qk(_map_stack(q_ref[:, cols]), load_k(cols))
            m, l, acc = _softmax_update(s, load_v(cols), m_scr[h], l_scr[h], acc_scr[h])
            m_scr[h] = m
            l_scr[h] = l
            acc_scr[h] = acc

    update(lambda cols: ck_ref[0, :, cols], lambda cols: cv_ref[0, :, cols])

    @pl.when(j == nk - 1)
    def _():
        update(lambda cols: kn_ref[:, cols], lambda cols: vn_ref[:, cols])
        lam = _diff_lambda(lam_ref[...], lam_init)
        for h in range(A_HEADS):
            o = _diff_finish(acc_scr[h], l_scr[h], lam, sg_ref[...], lam_init)
            o_ref[:, h * A_V_DIM:(h + 1) * A_V_DIM] = o.astype(BF16)


def _attn_sample(q, k_new, v_new, cache_k, cache_v, lam_p, sub_gain, lam_init, row0, batch, seq, tk=1024):
    past = cache_k.shape[1]
    assert seq == CHUNK and past % CHUNK == 0 and row0 % seq == 0
    tk = min(tk, past)
    nk = past // tk
    b0 = row0 // seq
    width = A_HEADS * A_V_DIM
    new = pl.BlockSpec((seq, width), lambda b, j: (b0 + b, 0))
    return pl.pallas_call(
        functools.partial(_attn_sample_kernel, lam_init=lam_init, nk=nk),
        grid=(batch, nk),
        in_specs=[
            pl.BlockSpec((4, A_QK_DIM), lambda b, j: (0, 0)),
            pl.BlockSpec((1, A_V_DIM), lambda b, j: (0, 0)),
            new, new, new,
            pl.BlockSpec((1, tk, width), lambda b, j: (b, j, 0)),
            pl.BlockSpec((1, tk, width), lambda b, j: (b, j, 0)),
        ],
        out_specs=pl.BlockSpec((seq, width), lambda b, j: (b, 0)),
        out_shape=jax.ShapeDtypeStruct((batch * seq, width), BF16),
        scratch_shapes=[pltpu.VMEM((A_HEADS, 2 * seq, LANES), F32), pltpu.VMEM((A_HEADS, 2 * seq, LANES), F32),
                        pltpu.VMEM((A_HEADS, 2 * seq, A_V_DIM), F32)],
        compiler_params=_params(("parallel", "arbitrary")),
        name="attn_sample",
    )(lam_p.astype(F32), sub_gain.reshape(1, A_V_DIM).astype(F32), q, k_new, v_new, cache_k, cache_v)


def _router_kernel(x_ref, gain_ref, w_ref, h_ref, ids_ref, gates_ref):
    x = x_ref[...]
    ms = jnp.mean(x * x, axis=-1, keepdims=True)
    h = x * lax.rsqrt(ms + EPS) * gain_ref[...]
    h_ref[...] = h
    logits = jnp.dot(h, w_ref[...], preferred_element_type=F32, precision=lax.Precision.HIGHEST)
    lane = lax.broadcasted_iota(jnp.int32, logits.shape, 1)
    gl = jnp.where(lane < N_GROUPS, logits, NEG)
    gm = jnp.max(gl, axis=-1, keepdims=True)
    g_w = 1.0 / jnp.sum(jnp.exp(gl - gm), axis=-1, keepdims=True)
    g_idx = jnp.min(jnp.where(gl == gm, lane, LANES), axis=-1, keepdims=True)
    first = N_GROUPS + g_idx * EXPERTS_PER_GROUP
    fl = jnp.where((lane >= first) & (lane < first + EXPERTS_PER_GROUP), logits, NEG)
    m1 = jnp.max(fl, axis=-1, keepdims=True)
    i1 = jnp.min(jnp.where(fl == m1, lane, LANES), axis=-1, keepdims=True)
    fl2 = jnp.where(lane == i1, NEG, fl)
    m2 = jnp.max(fl2, axis=-1, keepdims=True)
    i2 = jnp.min(jnp.where(fl2 == m2, lane, LANES), axis=-1, keepdims=True)
    e2 = jnp.exp(m2 - m1)
    w1 = 1.0 / (1.0 + e2)
    ids_ref[...] = jnp.where(lane == 0, i1 - N_GROUPS, jnp.where(lane == 1, i2 - N_GROUPS, 0))
    gates_ref[...] = jnp.where(lane == 0, g_w * w1, jnp.where(lane == 1, g_w * (e2 * w1), 0.0))


def _router(x, gain, w_rt, tm=512):
    M, D = x.shape
    tm = min(tm, M)
    return pl.pallas_call(
        _router_kernel,
        grid=(M // tm,),
        in_specs=[pl.BlockSpec((tm, D), lambda i: (i, 0)), pl.BlockSpec((1, D), lambda i: (0, 0)),
                  pl.BlockSpec((D, LANES), lambda i: (0, 0))],
        out_specs=[pl.BlockSpec((tm, D), lambda i: (i, 0)), pl.BlockSpec((tm, LANES), lambda i: (i, 0)),
                   pl.BlockSpec((tm, LANES), lambda i: (i, 0))],
        out_shape=[jax.ShapeDtypeStruct((M, D), F32), jax.ShapeDtypeStruct((M, LANES), jnp.int32),
                   jax.ShapeDtypeStruct((M, LANES), F32)],
        compiler_params=_params(("parallel",)),
        name="moe_router",
    )(x, gain.reshape(1, D).astype(F32), w_rt)


def _row_copy(src, src_row, dst, dst_row, sem):
    return pltpu.make_async_copy(src.at[pl.ds(src_row, 1), :], dst.at[pl.ds(dst_row, 1), :], sem)


def _dispatch_kernel(dest_ref, h_ref, xs_ref, sem, *, tm):
    base = pl.program_id(0) * tm

    def start(r, carry):
        for k in range(2):
            _row_copy(h_ref, base + r, xs_ref, dest_ref[0, 0, 2 * r + k], sem).start()
        return carry

    lax.fori_loop(0, tm, start, 0, unroll=ROW_DMA_UNROLL)
    pltpu.make_async_copy(h_ref.at[pl.ds(0, 2 * tm), :], xs_ref.at[pl.ds(0, 2 * tm), :], sem).wait()


def _dispatch(h, dest3, tm):
    M, D = h.shape
    return pl.pallas_call(
        functools.partial(_dispatch_kernel, tm=tm),
        grid=(M // tm,),
        in_specs=[pl.BlockSpec((1, 1, 2 * tm), lambda i: (i, 0, 0), memory_space=pltpu.SMEM),
                  pl.BlockSpec(memory_space=pl.ANY)],
        out_specs=pl.BlockSpec(memory_space=pl.ANY),
        out_shape=jax.ShapeDtypeStruct((2 * M, D), F32),
        scratch_shapes=[pltpu.SemaphoreType.DMA(())],
        compiler_params=_params(("arbitrary",)),
        name="moe_dispatch",
    )(dest3, h)


def _expert_weight_copies(layer, e, slot, wg_hbm, wu_hbm, wd_hbm, wg_buf, wu_buf, wd_buf, sems):
    return (pltpu.make_async_copy(wg_hbm.at[layer, e], wg_buf.at[slot], sems.at[slot, 0]),
            pltpu.make_async_copy(wu_hbm.at[layer, e], wu_buf.at[slot], sems.at[slot, 1]),
            pltpu.make_async_copy(wd_hbm.at[layer, e], wd_buf.at[slot], sems.at[slot, 2]))


def _experts_kernel(tile_ref, exp_ref, lo_ref, hi_ref, flag_ref, next_ref, n_ref,
                    x_ref, wg_hbm, wu_hbm, wd_hbm, o_ref,
                    wg_buf, wu_buf, wd_buf, wgu_scr, wd_scr, sems, *, layer):
    w = pl.program_id(0)
    live = w < n_ref[0]
    flags = flag_ref[w]
    slot = (flags >> 2) & 1
    copies = functools.partial(_expert_weight_copies, layer, wg_hbm=wg_hbm, wu_hbm=wu_hbm, wd_hbm=wd_hbm,
                               wg_buf=wg_buf, wu_buf=wu_buf, wd_buf=wd_buf, sems=sems)

    @pl.when(live & (w == 0))
    def _():
        for cp in copies(exp_ref[0], 0):
            cp.start()

    @pl.when(live & ((flags & 2) != 0))
    def _():
        for cp in copies(exp_ref[w], slot):
            cp.wait()

        @pl.when(next_ref[w] >= 0)
        def _():
            for cp in copies(next_ref[w], 1 - slot):
                cp.start()

        wgu_scr[:, :D_EXPERT] = wg_buf[slot].astype(BF16)
        wgu_scr[:, D_EXPERT:] = wu_buf[slot].astype(BF16)
        wd_scr[...] = wd_buf[slot].astype(BF16)

    @pl.when(live)
    def _():
        gu = jnp.dot(x_ref[...].astype(BF16), wgu_scr[...], preferred_element_type=F32)
        g = gu[:, :D_EXPERT]
        hid = g * jax.nn.sigmoid(g) * gu[:, D_EXPERT:]
        y = jnp.dot(hid.astype(BF16), wd_scr[...], preferred_element_type=F32)
        row = tile_ref[w] * MOE_TILE + lax.broadcasted_iota(jnp.int32, (MOE_TILE, 1), 0)
        y = jnp.where((row >= lo_ref[w]) & (row < hi_ref[w]), y, 0.0)

        @pl.when((flags & 1) != 0)
        def _():
            o_ref[...] = y

        @pl.when((flags & 1) == 0)
        def _():
            o_ref[...] += y


def _experts(xs, plan, w_gate, w_up, w_down, layer):
    A, D = xs.shape
    n_items = A // MOE_TILE + N_EXPERTS
    any_spec = pl.BlockSpec(memory_space=pl.ANY)
    grid_spec = pltpu.PrefetchScalarGridSpec(
        num_scalar_prefetch=7,
        grid=(n_items,),
        in_specs=[pl.BlockSpec((MOE_TILE, D), lambda w, tile, *_: (tile[w], 0)), any_spec, any_spec, any_spec],
        out_specs=pl.BlockSpec((MOE_TILE, D), lambda w, tile, *_: (tile[w], 0)),
        scratch_shapes=[pltpu.VMEM((2, D, D_EXPERT), F32), pltpu.VMEM((2, D, D_EXPERT), F32),
                        pltpu.VMEM((2, D_EXPERT, D), F32),
                        pltpu.VMEM((D, 2 * D_EXPERT), BF16), pltpu.VMEM((D_EXPERT, D), BF16),
                        pltpu.SemaphoreType.DMA((2, 3))],
    )
    return pl.pallas_call(
        functools.partial(_experts_kernel, layer=layer),
        grid_spec=grid_spec,
        out_shape=jax.ShapeDtypeStruct((A, D), F32),
        compiler_params=_params(("arbitrary",)),
        name="moe_experts",
    )(*plan, xs, w_gate, w_up, w_down)


def _combine_kernel(dest_ref, x_ref, g_ref, y_ref, o_ref, ybuf, sem, *, tm):
    def start(r, carry):
        for k in range(2):
            _row_copy(y_ref, dest_ref[0, 0, 2 * r + k], ybuf, k * tm + r, sem).start()
        return carry

    lax.fori_loop(0, tm, start, 0, unroll=ROW_DMA_UNROLL)
    pltpu.make_async_copy(y_ref.at[pl.ds(0, 2 * tm), :], ybuf, sem).wait()
    g = g_ref[...]
    o_ref[...] = x_ref[...] + g[:, 0:1] * ybuf[0:tm, :] + g[:, 1:2] * ybuf[tm:2 * tm, :]


def _combine(x, gates, y, dest3, tm):
    M, D = x.shape
    row = pl.BlockSpec((tm, D), lambda i: (i, 0))
    return pl.pallas_call(
        functools.partial(_combine_kernel, tm=tm),
        grid=(M // tm,),
        in_specs=[pl.BlockSpec((1, 1, 2 * tm), lambda i: (i, 0, 0), memory_space=pltpu.SMEM),
                  row, pl.BlockSpec((tm, LANES), lambda i: (i, 0)), pl.BlockSpec(memory_space=pl.ANY)],
        out_specs=row,
        out_shape=jax.ShapeDtypeStruct((M, D), F32),
        scratch_shapes=[pltpu.VMEM((2 * tm, D), F32), pltpu.SemaphoreType.DMA(())],
        compiler_params=_params(("arbitrary",)),
        name="moe_combine",
    )(dest3, x, gates, y)


def _route_plan(ids):
    flat = ids.reshape(-1)
    n_assign = flat.shape[0]
    blk = 128
    assert n_assign % blk == 0 and n_assign % MOE_TILE == 0
    onehot = (flat[:, None] == jnp.arange(N_EXPERTS, dtype=jnp.int32)[None, :]).astype(F32)
    oh3 = onehot.reshape(n_assign // blk, blk, N_EXPERTS)
    tri = (jnp.arange(blk)[:, None] >= jnp.arange(blk)[None, :]).astype(F32)
    local = jnp.einsum("ij,bje->bie", tri, oh3)
    tot = local[:, -1, :]
    before = jnp.cumsum(tot, axis=0) - tot
    counts = jnp.sum(tot, axis=0).astype(jnp.int32)
    starts = jnp.cumsum(counts) - counts
    base = before + starts.astype(F32)[None, :] - 1.0
    dest = jnp.sum(oh3 * (local + base[:, None, :]), axis=-1).astype(jnp.int32).reshape(-1, 2)

    ends = starts + counts
    first_tile = starts // MOE_TILE
    n_it = jnp.where(counts > 0, (ends - 1) // MOE_TILE - first_tile + 1, 0)
    it_end = jnp.cumsum(n_it)
    it_start = it_end - n_it
    n_items = it_end[-1]
    n_slots = n_assign // MOE_TILE + N_EXPERTS
    w = jnp.arange(n_slots, dtype=jnp.int32)
    wc = jnp.minimum(w, n_items - 1)
    e_w = jnp.sum((it_end[None, :] <= wc[:, None]).astype(jnp.int32), axis=1)
    ordinal = jnp.cumsum((counts > 0).astype(jnp.int32)) - 1
    idx = jnp.arange(N_EXPERTS, dtype=jnp.int32)
    later = jnp.where((counts[None, :] > 0) & (idx[None, :] > idx[:, None]), idx[None, :], N_EXPERTS)
    nxt = jnp.min(later, axis=1)
    nxt = jnp.where(nxt == N_EXPERTS, -1, nxt)
    table = jnp.stack([first_tile, it_start, starts, ends, ordinal % 2, nxt], axis=1).astype(jnp.int32)
    sel = (e_w[:, None] == idx[None, :]).astype(jnp.int32)
    ft_w, is_w, lo_w, hi_w, par_w, nxt_w = jnp.moveaxis(jnp.sum(sel[:, :, None] * table[None], axis=1), 1, 0)
    tile_w = ft_w + wc - is_w
    prev_tile = jnp.concatenate([jnp.full((1,), -1, jnp.int32), tile_w[:-1]])
    prev_e = jnp.concatenate([jnp.full((1,), -1, jnp.int32), e_w[:-1]])
    live = w < n_items
    flags = ((tile_w != prev_tile) & live).astype(jnp.int32) + 2 * ((e_w != prev_e) & live).astype(jnp.int32) \
        + 4 * par_w
    plan = (tile_w, e_w, lo_w, hi_w, flags, nxt_w, n_items.reshape(1).astype(jnp.int32))
    return dest, plan


def _moe(x, gain, w_group, w_router, w_gate, w_up, w_down, layer, tm=256):
    M, D = x.shape
    tm = min(tm, M)
    pad = jnp.zeros((D, LANES - N_GROUPS - N_EXPERTS), F32)
    w_rt = jnp.concatenate([w_group.astype(F32), w_router.astype(F32), pad], axis=1)
    h, ids, gates = _router(x, gain, w_rt)
    dest, plan = _route_plan(ids[:, :2])
    dest3 = dest.reshape(M // tm, 1, 2 * tm)
    xs = _dispatch(h, dest3, tm)
    y = _experts(xs, plan, w_gate, w_up, w_down, layer)
    return _combine(x, gates, y, dest3, tm)


def _dotb(a, b):
    return jnp.dot(a.astype(BF16), b.astype(BF16), preferred_element_type=F32)


def _delta_kernel(uq_ref, uk_ref, uv_ref, z_ref, wq_ref, wk_ref, wv_ref, hq_ref, hk_ref, hv_ref,
                  gt_ref, hp_ref, og_ref, s0_ref, o_ref, s_ref, ext, *, kheads, first_of):
    L = CHUNK
    P = G_PACK * L
    qw = kheads * G_DIM
    first = first_of(pl.program_id(1))

    @pl.when(first)
    def _():
        s_ref[...] = s0_ref[...]
        ext[8 - (G_CONV - 1):8, 0:qw] = hq_ref[0]
        ext[8 - (G_CONV - 1):8, qw:2 * qw] = hk_ref[0]
        ext[8 - (G_CONV - 1):8, 2 * qw:] = hv_ref[0]

    ext[8:, 0:qw] = uq_ref[...].astype(F32)
    ext[8:, qw:2 * qw] = uk_ref[...].astype(F32)
    ext[8:, 2 * qw:] = uv_ref[...].astype(F32)
    wc = jnp.concatenate([wq_ref[...], wk_ref[...], wv_ref[...]], axis=1)
    y = ext[8:8 + L, :] * wc[G_CONV - 1:G_CONV, :]
    for j in range(G_CONV - 1):
        y = y + ext[5 + j:5 + j + L, :] * wc[j:j + 1, :]
    y = y * jax.nn.sigmoid(y)
    ext[8 - (G_CONV - 1):8, :] = ext[8 + L - (G_CONV - 1):8 + L, :]

    ri = lax.broadcasted_iota(jnp.int32, (P, P), 0)
    ci = lax.broadcasted_iota(jnp.int32, (P, P), 1)
    same_head = (ri // L) == (ci // L)
    eye = ri == ci
    incl = same_head & (ri >= ci)
    strict = same_head & (ri > ci)
    last = same_head & ((ci % L) == L - 1)
    sub_diag = (ri // G_SUB) == (ci // G_SUB)
    zero_blk = jnp.zeros((L, G_DIM), F32)

    def l2n(x):
        return x * lax.rsqrt(jnp.sum(x * x, axis=-1, keepdims=True) + EPS)

    def col_of(mask, row):
        return jnp.sum(jnp.where(mask, row, 0.0), axis=1, keepdims=True)

    rep = G_V_HEADS // G_K_HEADS
    for g in range(rep * kheads // G_PACK):
        heads = range(G_PACK * g, G_PACK * (g + 1))
        qh = {hk: l2n(y[:, hk * G_DIM:(hk + 1) * G_DIM]) * (G_DIM ** -0.5) for hk in {j // rep for j in heads}}
        kh = {hk: l2n(y[:, qw + hk * G_DIM:qw + (hk + 1) * G_DIM]) for hk in qh}
        q_all = jnp.concatenate([qh[j // rep] for j in heads], axis=0)
        k_all = jnp.concatenate([kh[j // rep] for j in heads], axis=0)
        v_all = jnp.concatenate([y[:, 2 * qw + j * G_DIM:2 * qw + (j + 1) * G_DIM] for j in heads], axis=0)
        kb = k_all.astype(BF16)
        qkk = _qk(jnp.concatenate([q_all.astype(BF16), kb], axis=0), kb)

        beta_row = jax.nn.sigmoid(gt_ref[0, g, 0:1, :])
        x = gt_ref[0, g, 1:2, :] + hp_ref[g, 1:2, :]
        softplus = jnp.maximum(x, 0.0) + jnp.log(1.0 + jnp.exp(-jnp.abs(x)))
        g_row = -jnp.exp(hp_ref[g, 0:1, :]) * softplus
        gc_col = col_of(incl, g_row)
        gc_row = jnp.sum(jnp.where(eye, gc_col, 0.0), axis=0, keepdims=True)
        gl_col = col_of(last, gc_row)
        beta_col = col_of(eye, beta_row)
        decay = jnp.where(incl, jnp.exp(jnp.where(incl, gc_col - gc_row, 0.0)), 0.0)
        a_mat = jnp.where(strict, beta_col * qkk[P:] * decay, 0.0)
        n1 = jnp.where(sub_diag, -a_mat, 0.0)
        a_off = jnp.where(sub_diag, 0.0, a_mat)
        n2 = _dotb(n1, n1)
        t = _dotb(jnp.concatenate([n1, n2], axis=0), n2)
        r3 = n1 + n2 + t[:P]
        n4 = t[P:]
        t = _dotb(jnp.concatenate([r3, n4], axis=0), n4)
        r7 = r3 + n4 + t[:P]
        n8 = t[P:]
        rd = r7 + n8 + _dotb(r7, n8)
        exp_gc = jnp.exp(gc_col)
        rhs = jnp.concatenate([v_all * beta_col, k_all * (beta_col * exp_gc)], axis=1)
        t = _dotb(rd, jnp.concatenate([a_off, rhs], axis=1))
        mb = (a_off + t[:, :P]).astype(BF16)
        y1 = rhs + t[:, P:]
        uw = y1
        for _ in range(L // G_SUB - 1):
            uw = y1 - jnp.dot(mb, uw.astype(BF16), preferred_element_type=F32)
        u = uw[:, :G_DIM]
        wmat = uw[:, G_DIM:]
        q_dec = q_all * exp_gc
        k_dec = k_all * jnp.exp(gl_col - gc_col)
        s_cat = jnp.concatenate([s_ref[0, j] for j in heads], axis=1)
        wq_s = _dotb(jnp.concatenate([wmat, q_dec], axis=0), s_cat)
        own = lambda base: jnp.concatenate(
            [wq_s[base + h * L:base + (h + 1) * L, h * G_DIM:(h + 1) * G_DIM] for h in range(G_PACK)], axis=0)
        v_new = u - own(0)
        o = own(P) + _dotb(qkk[:P] * decay, v_new)
        v_wide = jnp.concatenate(
            [jnp.concatenate([v_new[h * L:(h + 1) * L] if hh == h else zero_blk for hh in range(G_PACK)], axis=1)
             for h in range(G_PACK)], axis=0)
        keep = jnp.concatenate(
            [jnp.broadcast_to(jnp.exp(gc_row[:, (h + 1) * L - 1:(h + 1) * L]), (1, G_DIM)) for h in range(G_PACK)],
            axis=1)
        s_cat = s_cat * keep + _dotb(k_dec.T, v_wide)
        ms = jnp.mean(o * o, axis=-1, keepdims=True)
        z = jnp.concatenate([z_ref[:, j * G_DIM:(j + 1) * G_DIM] for j in heads], axis=0).astype(F32)
        o = o * lax.rsqrt(ms + EPS) * og_ref[...] * (z * jax.nn.sigmoid(z))
        for h, j in enumerate(heads):
            s_ref[0, j] = s_cat[:, h * G_DIM:(h + 1) * G_DIM]
            o_ref[:, j * G_DIM:(j + 1) * G_DIM] = o[h * L:(h + 1) * L].astype(BF16)


def _delta(u, z, gates, head_params, conv_w, hist, out_gain, s0, groups, kheads=4):
    (b0, t0), (b1, t1) = groups
    n0, n1 = t0 // CHUNK, t1 // CHUNK
    nb0 = b0 * n0
    n_blocks = nb0 + b1 * n1
    rep = G_V_HEADS // G_K_HEADS
    vh = rep * kheads
    qw, vw = kheads * G_DIM, vh * G_DIM
    n_hg = G_K_HEADS // kheads
    qk_w = G_K_HEADS * G_DIM

    def seq_of(s):
        return jnp.where(s < nb0, s // n0, b0 + (s - nb0) // n1)

    def first_of(s):
        return jnp.where(s < nb0, s % n0 == 0, (s - nb0) % n1 == 0)

    cols = lambda width, off: pl.BlockSpec((CHUNK, width), lambda h, s: (s, off + h))
    wspec = lambda width, off: pl.BlockSpec((G_CONV, width), lambda h, s: (0, off + h))
    hspec = lambda width, off: pl.BlockSpec((1, G_CONV - 1, width), lambda h, s: (seq_of(s), 0, off + h))
    assert vh % G_PACK == 0
    packs = vh // G_PACK
    pw = G_PACK * CHUNK
    state = pl.BlockSpec((1, vh, G_DIM, G_DIM), lambda h, s: (seq_of(s), h, 0, 0))
    return pl.pallas_call(
        functools.partial(_delta_kernel, kheads=kheads, first_of=first_of),
        grid=(n_hg, n_blocks),
        in_specs=[
            cols(qw, 0), cols(qw, qk_w // qw), cols(vw, 2 * qk_w // vw), cols(vw, 0),
            wspec(qw, 0), wspec(qw, qk_w // qw), wspec(vw, 2 * qk_w // vw),
            hspec(qw, 0), hspec(qw, qk_w // qw), hspec(vw, 2 * qk_w // vw),
            pl.BlockSpec((1, packs, 2, pw), lambda h, s: (s, h, 0, 0)),
            pl.BlockSpec((packs, 2, pw), lambda h, s: (h, 0, 0)),
            pl.BlockSpec((1, G_DIM), lambda h, s: (0, 0)),
            state,
        ],
        out_specs=[cols(vw, 0), state],
        out_shape=[jax.ShapeDtypeStruct((u.shape[0], G_V_HEADS * G_DIM), BF16),
                   jax.ShapeDtypeStruct(s0.shape, F32)],
        scratch_shapes=[pltpu.VMEM((8 + CHUNK, 2 * qw + vw), F32)],
        compiler_params=_params(("parallel", "arbitrary")),
        name="gdn_delta",
    )(u, u, u, z, conv_w, conv_w, conv_w, hist, hist, hist, gates, head_params,
      out_gain.reshape(1, G_DIM).astype(F32), s0)


def _gdn(x, groups, hist, s0, gain, w_in, conv_w, a_log, dt_bias, out_gain, w_out):
    M, D = x.shape
    qkv_w = 2 * G_K_HEADS * G_DIM + G_V_HEADS * G_DIM
    v_w = G_V_HEADS * G_DIM
    w_qkv = w_in[:, :qkv_w].astype(BF16)
    w_z = w_in[:, qkv_w:qkv_w + v_w].astype(BF16)
    w_ba = jnp.pad(w_in[:, qkv_w + v_w:], ((0, 0), (0, LANES - 2 * G_V_HEADS))).astype(BF16)
    u = _mm(x, w_qkv, gain=gain, out_dtype=BF16)
    z = _mm(x, w_z, gain=gain, out_dtype=BF16)
    ba = _mm(x, w_ba, gain=gain)
    n_packs = G_V_HEADS // G_PACK
    pw = G_PACK * CHUNK
    bat = jnp.swapaxes(ba.reshape(M // CHUNK, CHUNK, LANES), 1, 2)[:, :2 * G_V_HEADS]
    gates = jnp.swapaxes(bat.reshape(M // CHUNK, 2, n_packs, pw), 1, 2)
    per_head = lambda p: jnp.broadcast_to(p.astype(F32).reshape(n_packs, G_PACK, 1), (n_packs, G_PACK, CHUNK))
    head_params = jnp.stack([per_head(a_log).reshape(n_packs, pw), per_head(dt_bias).reshape(n_packs, pw)], axis=1)
    o, s_new = _delta(u, z, gates, head_params, conv_w.astype(F32), hist, out_gain, s0, groups)
    return _mm(o, w_out.astype(BF16), res=x), u, s_new


def _diff_attn(x, prompt, sample, cache_k, cache_v, gain, w_in, q_gain, k_gain, lam_p, sub_gain, w_out,
               lam_init):
    qk_w = A_HEADS * 2 * A_QK_DIM
    scale = A_QK_DIM ** -0.5 * LOG2E
    tile = lambda g: jnp.tile(g.astype(F32).reshape(-1), A_HEADS)
    q = _mm(x, w_in[:, :qk_w].astype(BF16), gain=gain, gn_gain=tile(q_gain) * scale, out_dtype=BF16)
    k, kb = _mm(x, w_in[:, qk_w:2 * qk_w].astype(BF16), gain=gain, gn_gain=tile(k_gain), out2_dtype=BF16)
    v, vb = _mm(x, w_in[:, 2 * qk_w:].astype(BF16), gain=gain, out2_dtype=BF16)
    (bp, tp), (bs, ts) = prompt, sample
    mp = bp * tp
    op = _attn_prompt(q, kb, vb, lam_p, sub_gain, lam_init, bp, tp)
    past = cache_k.shape[1]
    ck = cache_k.reshape(bs, past, qk_w).astype(BF16)
    cv = cache_v.reshape(bs, past, A_HEADS * A_V_DIM).astype(BF16)
    o_s = _attn_sample(q, kb, vb, ck, cv, lam_p, sub_gain, lam_init, mp, bs, ts)
    o = jnp.concatenate([op, o_s], axis=0)
    return _mm(o, w_out.astype(BF16), res=x), k, v


def kernel(x_prompt, x_sample, cache_k_attn, cache_v_attn, state_conv_delta, state_ssm_delta, norm_mix, norm_ffn, attn_w_in, attn_q_norm, attn_k_norm, attn_lambda, attn_sub_norm, attn_w_out, delta_w_in, delta_conv, delta_a_log, delta_dt_bias, delta_out_norm, delta_w_out, moe_w_group, moe_w_router, moe_w_gate, moe_w_up, moe_w_down):
    bp, tp, d = x_prompt.shape
    bs, ts, _ = x_sample.shape
    mp = bp * tp
    x = jnp.concatenate([x_prompt.reshape(mp, d), x_sample.reshape(bs * ts, d)], axis=0).astype(F32)

    def moe(x, i):
        return _moe(x, norm_ffn[i], moe_w_group[i], moe_w_router[i], moe_w_gate, moe_w_up, moe_w_down, i)

    lam_init = 0.8 - 0.6 * math.exp(-0.3 * 0)
    x, k, v = _diff_attn(x, (bp, tp), (bs, ts), cache_k_attn[0], cache_v_attn[0], norm_mix[0], attn_w_in[0],
                         attn_q_norm[0], attn_k_norm[0], attn_lambda[0], attn_sub_norm[0], attn_w_out[0], lam_init)
    x = moe(x, 0)
    conv_c = state_conv_delta.shape[-1]
    hist = jnp.concatenate([jnp.zeros((bp, G_CONV - 1, conv_c), F32), state_conv_delta[0].astype(F32)], axis=0)
    s0 = jnp.concatenate([jnp.zeros((bp,) + state_ssm_delta.shape[2:], F32), state_ssm_delta[0].astype(F32)], axis=0)
    x, u, s_new = _gdn(x, [(bp, tp), (bs, ts)], hist, s0, norm_mix[1], delta_w_in[0], delta_conv[0],
                       delta_a_log[0], delta_dt_bias[0], delta_out_norm[0], delta_w_out[0])
    x = moe(x, 1)

    assert tp >= G_CONV - 1 and ts >= G_CONV - 1
    def tail(row0, b, t):
        ends = [row0 + (i + 1) * t for i in range(b)]
        return jnp.stack([u[e - (G_CONV - 1):e] for e in ends]).astype(F32)[None]

    kv_p = lambda a, last: a[:mp].reshape((1, bp, tp, A_HEADS) + last)
    kv_s = lambda a, last: a[mp:].reshape((1, bs, ts, A_HEADS) + last)
    return (x[:mp].reshape(bp, tp, d), x[mp:].reshape(bs, ts, d),
            kv_p(k, (2, A_QK_DIM)), kv_p(v, (A_V_DIM,)), kv_s(k, (2, A_QK_DIM)), kv_s(v, (A_V_DIM,)),
            tail(0, bp, tp), s_new[:bp][None], tail(mp, bs, ts), s_new[bp:][None])
```
